```python
import jax, jax.numpy as jnp
from jax import lax
import numpy as np

D_MODEL = 1024
BATCH = 2
SEQ = 8192
DEPTH = 1

MIX_WIDTH = D_MODEL
POOL_WIDTH = MIX_WIDTH // 2
CONV_WIDTH = MIX_WIDTH - POOL_WIDTH
POOL_WINDOWS = (2, 4, 8, 16)
N_POOL_GROUPS = len(POOL_WINDOWS)
POOL_GROUP = POOL_WIDTH // N_POOL_GROUPS
CONV_HEADS = 8
CONV_K = 3
IN_COLS = 2 * POOL_WIDTH + 4 * CONV_WIDTH
EPS = 1e-6

kernel_name = "hybrid_pool_shortconv_block"


def rmsnorm(x, g):
    xf = x.astype(jnp.float32)
    y = xf * lax.rsqrt(jnp.mean(xf * xf, axis=-1, keepdims=True) + EPS)
    return y.astype(x.dtype) * g


def centred_mean(u, w):
    s = u.shape[1]
    cs = jnp.cumsum(u.astype(jnp.float32), axis=1)
    cs = jnp.pad(cs, ((0, 0), (1, 0), (0, 0)))
    t = jnp.arange(s)
    lo = jnp.clip(t - w // 2, 0, s)
    hi = jnp.clip(t + (w - w // 2), 0, s)
    total = jnp.take(cs, hi, axis=1) - jnp.take(cs, lo, axis=1)
    cnt = (hi - lo).astype(jnp.float32)
    return (total / cnt[None, :, None]).astype(u.dtype)


def pool_mixer(u, pool_w, pool_scale):
    b, s, _ = u.shape
    ug = u.reshape(b, s, N_POOL_GROUPS, POOL_GROUP)
    pooled = jnp.stack(
        [centred_mean(ug[:, :, g, :], w) for g, w in enumerate(POOL_WINDOWS)], axis=2
    ) - ug
    mixed = jnp.einsum('bsgc,gcd->bsgd', pooled, pool_w)
    return mixed.reshape(b, s, POOL_WIDTH) * pool_scale


def gated_short_conv(b_gate, c_gate, v, conv_w, conv_b):
    s = v.shape[1]
    cv = c_gate * v
    p = jnp.pad(cv, ((0, 0), (1, 1), (0, 0)))
    conv = p[:, 0:s] * conv_w[0] + p[:, 1:s + 1] * conv_w[1] + p[:, 2:s + 2] * conv_w[2] + conv_b
    return b_gate * conv


def setup_inputs(seed: int = 0) -> dict:
    key = jax.random.key(seed)
    ks = jax.random.split(key, 10)
    f = jnp.float32
    x = jax.random.normal(ks[0], (BATCH, SEQ, D_MODEL), f)
    norm_g = 1.0 + 0.02 * jax.random.normal(ks[1], (DEPTH, D_MODEL), f)
    w_in = jax.random.normal(ks[2], (DEPTH, D_MODEL, IN_COLS), f) * D_MODEL ** -0.5
    pool_w = jax.random.normal(ks[3], (DEPTH, N_POOL_GROUPS, POOL_GROUP, POOL_GROUP), f) * POOL_GROUP ** -0.5
    pool_scale = 1.0 + 0.1 * jax.random.normal(ks[4], (DEPTH, POOL_WIDTH), f)
    conv_w = jax.random.normal(ks[5], (DEPTH, CONV_K, CONV_WIDTH), f) * CONV_K ** -0.5
    conv_b = 0.01 * jax.random.normal(ks[6], (DEPTH, CONV_WIDTH), f)
    w_out = jax.random.normal(ks[7], (DEPTH, MIX_WIDTH, D_MODEL), f) * MIX_WIDTH ** -0.5
    final_g = 1.0 + 0.02 * jax.random.normal(ks[8], (D_MODEL,), f)
    return {"x": x, "norm_g": norm_g, "w_in": w_in, "pool_w": pool_w,
            "pool_scale": pool_scale, "conv_w": conv_w, "conv_b": conv_b,
            "w_out": w_out, "final_g": final_g}


def reference(x, norm_g, w_in, pool_w, pool_scale, conv_w, conv_b, w_out, final_g):
    splits = np.cumsum([POOL_WIDTH, POOL_WIDTH, CONV_WIDTH, CONV_WIDTH, CONV_WIDTH]).tolist()
    for l in range(DEPTH):
        h = rmsnorm(x, norm_g[l])
        proj = jnp.einsum('bsd,de->bse', h, w_in[l])
        u_a, z_a, b_g, c_g, v_b, z_b = jnp.split(proj, splits, axis=-1)
        y_a = pool_mixer(u_a, pool_w[l], pool_scale[l]) * jax.nn.silu(z_a)
        y_b = gated_short_conv(b_g, c_g, v_b, conv_w[l], conv_b[l]) * jax.nn.silu(z_b)
        y = jnp.concatenate([y_a, y_b], axis=-1)
        x = x + jnp.einsum('bse,ed->bsd', y, w_out[l])
    return rmsnorm(x, final_g)
```

```python
import functools

import jax
import jax.numpy as jnp
from jax import lax
from jax.experimental import pallas as pl
from jax.experimental.pallas import tpu as pltpu

POOL_WINDOWS = (2, 4, 8, 16)
EPS = 1e-6

LANES = 128
BF16_SUBLANES = 16
V7X_VMEM_BYTES = 64 * 1024 * 1024
SEQ_TILE = 512
HALO = BF16_SUBLANES


def _rmsnorm(xf, g):
    ms = jnp.mean(xf * xf, axis=-1, keepdims=True)
    return xf * lax.rsqrt(ms + EPS) * g


def _silu(z):
    return z * jax.nn.sigmoid(z)


def _layer_kernel(x_ref, xp_ref, xn_ref, ng_ref, win_ref, pw_ref, ps_ref,
                  cw_ref, cb_ref, wout_ref, fg_ref, o_ref, h_ref, *,
                  ts, seq, pool_w, apply_final):
    f32, bf16 = jnp.float32, jnp.bfloat16
    i = pl.program_id(1)
    n_tiles = pl.num_programs(1)
    n_ext = ts + 2 * HALO
    group = pool_w // len(POOL_WINDOWS)

    g = ng_ref[...]
    x = x_ref[0]
    has_prev = (i > 0).astype(f32)
    has_next = (i < n_tiles - 1).astype(f32)
    h_ref[0:HALO] = (_rmsnorm(xp_ref[0], g) * has_prev).astype(bf16)
    h_ref[HALO:HALO + ts] = _rmsnorm(x, g).astype(bf16)
    h_ref[HALO + ts:n_ext] = (_rmsnorm(xn_ref[0], g) * has_next).astype(bf16)

    def proj(h, col):
        return jnp.dot(h, win_ref[:, col:col + pool_w],
                       preferred_element_type=f32)

    h_ext = h_ref[...]
    h_main = h_ref[HALO:HALO + ts]
    u = proj(h_ext, 0)
    z_a = proj(h_main, pool_w)
    b_g = proj(h_main, 2 * pool_w)
    c_g = proj(h_ext, 3 * pool_w)
    v_b = proj(h_ext, 4 * pool_w)
    z_b = proj(h_main, 5 * pool_w)

    def shifted(a, k):
        return pltpu.roll(a, (-k) % n_ext, axis=0)

    t = (i * ts + lax.broadcasted_iota(jnp.int32, (ts, group), 0)).astype(f32)
    pooled = []
    for gi, w in enumerate(POOL_WINDOWS):
        ug = u[:, gi * group:(gi + 1) * group]
        half = w // 2
        a = ug
        s = 1
        while s < half:
            a = a + shifted(a, s)
            s *= 2
        tot = shifted(a, -half) + a
        cnt = jnp.minimum(t + half, float(seq)) - jnp.maximum(t - half, 0.0)
        pooled.append(tot[HALO:HALO + ts] / cnt - ug[HALO:HALO + ts])
    pooled = jnp.concatenate(pooled, axis=-1).astype(bf16)
    pair = 2 * group
    mixed = jnp.concatenate(
        [jnp.dot(pooled[:, p * pair:(p + 1) * pair], pw_ref[p],
                 preferred_element_type=f32)
         for p in range(pool_w // pair)], axis=-1)
    y_a = mixed * ps_ref[...] * _silu(z_a)

    cv = c_g * v_b
    conv = (shifted(cv, -1) * cw_ref[0:1] + cv * cw_ref[1:2]
            + shifted(cv, 1) * cw_ref[2:3] + cb_ref[...])
    y_b = b_g * conv[HALO:HALO + ts] * _silu(z_b)

    out = x + (jnp.dot(y_a.astype(bf16), wout_ref[0:pool_w],
                       preferred_element_type=f32)
               + jnp.dot(y_b.astype(bf16), wout_ref[pool_w:],
                         preferred_element_type=f32))
    if apply_final:
        out = _rmsnorm(out, fg_ref[...])
    o_ref[0] = out


def _pair_block_diag(pool_w):
    n_groups, c, _ = pool_w.shape
    pw = pool_w.reshape(n_groups // 2, 2, c, c)
    z = jnp.zeros_like(pw[:, 0])
    top = jnp.concatenate([pw[:, 0], z], axis=-1)
    bot = jnp.concatenate([z, pw[:, 1]], axis=-1)
    return jnp.concatenate([top, bot], axis=-2)


def _layer(x, norm_g, w_in, pool_w, pool_scale, conv_w, conv_b, w_out,
           final_g, *, apply_final):
    batch, seq, d_model = x.shape
    in_cols = w_in.shape[1]
    mix_w = w_out.shape[0]
    pool_width = pool_scale.shape[0]
    conv_width = conv_b.shape[0]
    ts = SEQ_TILE
    assert seq % ts == 0 and ts % HALO == 0
    assert pool_width == conv_width and in_cols == 6 * pool_width
    assert mix_w == pool_width + conv_width
    assert pool_width % (len(POOL_WINDOWS) * LANES) == 0
    assert max(POOL_WINDOWS) // 2 <= HALO
    blocks_per_tile = ts // HALO
    n_halo_blocks = seq // HALO

    bf16 = jnp.bfloat16
    const = lambda *shape: pl.BlockSpec(
        shape, lambda b, i: (0,) * len(shape), pipeline_mode=pl.Buffered(1))
    kernel = functools.partial(_layer_kernel, ts=ts, seq=seq,
                               pool_w=pool_width, apply_final=apply_final)
    return pl.pallas_call(
        kernel,
        out_shape=jax.ShapeDtypeStruct(x.shape, x.dtype),
        grid=(batch, seq // ts),
        in_specs=[
            pl.BlockSpec((1, ts, d_model), lambda b, i: (b, i, 0)),
            pl.BlockSpec((1, HALO, d_model), lambda b, i: (
                b, jnp.maximum(i * blocks_per_tile - 1, 0), 0)),
            pl.BlockSpec((1, HALO, d_model), lambda b, i: (
                b, jnp.minimum((i + 1) * blocks_per_tile, n_halo_blocks - 1), 0)),
            const(1, d_model),
            const(d_model, in_cols),
            const(len(POOL_WINDOWS) // 2, 2 * pool_width // len(POOL_WINDOWS),
                  2 * pool_width // len(POOL_WINDOWS)),
            const(1, pool_width),
            const(3, conv_width),
            const(1, conv_width),
            const(mix_w, d_model),
            const(1, d_model),
        ],
        out_specs=pl.BlockSpec((1, ts, d_model), lambda b, i: (b, i, 0)),
        scratch_shapes=[pltpu.VMEM((ts + 2 * HALO, d_model), bf16)],
        compiler_params=pltpu.CompilerParams(
            dimension_semantics=("arbitrary", "arbitrary"),
            vmem_limit_bytes=V7X_VMEM_BYTES * 7 // 8),
        name="hybrid_block_layer",
    )(x, x, x, norm_g.reshape(1, -1), w_in.astype(bf16),
      _pair_block_diag(pool_w).astype(bf16), pool_scale.reshape(1, -1),
      conv_w, conv_b.reshape(1, -1), w_out.astype(bf16),
      final_g.reshape(1, -1))


def kernel(x, norm_g, w_in, pool_w, pool_scale, conv_w, conv_b, w_out, final_g):
    depth = norm_g.shape[0]
    for l in range(depth):
        x = _layer(x, norm_g[l], w_in[l], pool_w[l], pool_scale[l], conv_w[l],
                   conv_b[l], w_out[l], final_g, apply_final=(l == depth - 1))
    return x
```

```python
import functools

import jax
import jax.numpy as jnp
from jax import lax
from jax.experimental import pallas as pl
from jax.experimental.pallas import tpu as pltpu

POOL_WINDOWS = (2, 4, 8, 16)
EPS = 1e-6

LANES = 128
BF16_SUBLANES = 16
V7X_VMEM_BYTES = 64 * 1024 * 1024
SEQ_TILE = 1024
HALO = BF16_SUBLANES


def _rmsnorm(xf, g):
    ms = jnp.mean(xf * xf, axis=-1, keepdims=True)
    return xf * lax.rsqrt(ms + EPS) * g


def _silu(z):
    return z * jax.nn.sigmoid(z)


def _layer_kernel(x_ref, xp_ref, xn_ref, ng_ref, win_ref, pw_ref, ps_ref,
                  cw_ref, cb_ref, wout_ref, fg_ref, o_ref, h_ref, *,
                  ts, seq, pool_w, apply_final):
    f32, bf16 = jnp.float32, jnp.bfloat16
    i = pl.program_id(1)
    n_tiles = pl.num_programs(1)
    n_ext = ts + 2 * HALO
    group = pool_w // len(POOL_WINDOWS)

    g = ng_ref[...]
    x = x_ref[0]
    has_prev = (i > 0).astype(f32)
    has_next = (i < n_tiles - 1).astype(f32)
    h_ref[0:HALO] = (_rmsnorm(xp_ref[0], g) * has_prev).astype(bf16)
    h_ref[HALO:HALO + ts] = _rmsnorm(x, g).astype(bf16)
    h_ref[HALO + ts:n_ext] = (_rmsnorm(xn_ref[0], g) * has_next).astype(bf16)

    def proj(h, col):
        return jnp.dot(h, win_ref[:, col:col + pool_w],
                       preferred_element_type=f32)

    h_ext = h_ref[...]
    h_main = h_ref[HALO:HALO + ts]
    u = proj(h_ext, 0)
    z_a = proj(h_main, pool_w)
    b_g = proj(h_main, 2 * pool_w)
    c_g = proj(h_ext, 3 * pool_w)
    v_b = proj(h_ext, 4 * pool_w)
    z_b = proj(h_main, 5 * pool_w)

    def shifted(a, k):
        return pltpu.roll(a, (-k) % n_ext, axis=0)

    t = (i * ts + lax.broadcasted_iota(jnp.int32, (ts, group), 0)).astype(f32)
    pooled = []
    for gi, w in enumerate(POOL_WINDOWS):
        ug = u[:, gi * group:(gi + 1) * group]
        half = w // 2
        a = ug
        s = 1
        while s < half:
            a = a + shifted(a, s)
            s *= 2
        tot = shifted(a, -half) + a
        cnt = jnp.minimum(t + half, float(seq)) - jnp.maximum(t - half, 0.0)
        pooled.append(tot[HALO:HALO + ts] / cnt - ug[HALO:HALO + ts])
    pooled = jnp.concatenate(pooled, axis=-1).astype(bf16)
    pair = 2 * group
    mixed = jnp.concatenate(
        [jnp.dot(pooled[:, p * pair:(p + 1) * pair], pw_ref[p],
                 preferred_element_type=f32)
         for p in range(pool_w // pair)], axis=-1)
    y_a = mixed * ps_ref[...] * _silu(z_a)

    cv = c_g * v_b
    conv = (shifted(cv, -1) * cw_ref[0:1] + cv * cw_ref[1:2]
            + shifted(cv, 1) * cw_ref[2:3] + cb_ref[...])
    y_b = b_g * conv[HALO:HALO + ts] * _silu(z_b)

    out = x + (jnp.dot(y_a.astype(bf16), wout_ref[0:pool_w],
                       preferred_element_type=f32)
               + jnp.dot(y_b.astype(bf16), wout_ref[pool_w:],
                         preferred_element_type=f32))
    if apply_final:
        out = _rmsnorm(out, fg_ref[...])
    o_ref[0] = out


def _pair_block_diag(pool_w):
    n_groups, c, _ = pool_w.shape
    pw = pool_w.reshape(n_groups // 2, 2, c, c)
    z = jnp.zeros_like(pw[:, 0])
    top = jnp.concatenate([pw[:, 0], z], axis=-1)
    bot = jnp.concatenate([z, pw[:, 1]], axis=-1)
    return jnp.concatenate([top, bot], axis=-2)


def _layer(x, norm_g, w_in, pool_w, pool_scale, conv_w, conv_b, w_out,
           final_g, *, apply_final):
    batch, seq, d_model = x.shape
    in_cols = w_in.shape[1]
    mix_w = w_out.shape[0]
    pool_width = pool_scale.shape[0]
    conv_width = conv_b.shape[0]
    ts = SEQ_TILE
    assert seq % ts == 0 and ts % HALO == 0
    assert pool_width == conv_width and in_cols == 6 * pool_width
    assert mix_w == pool_width + conv_width
    assert pool_width % (len(POOL_WINDOWS) * LANES) == 0
    assert max(POOL_WINDOWS) // 2 <= HALO
    blocks_per_tile = ts // HALO
    n_halo_blocks = seq // HALO

    bf16 = jnp.bfloat16
    const = lambda *shape: pl.BlockSpec(
        shape, lambda b, i: (0,) * len(shape), pipeline_mode=pl.Buffered(1))
    kernel = functools.partial(_layer_kernel, ts=ts, seq=seq,
                               pool_w=pool_width, apply_final=apply_final)
    return pl.pallas_call(
        kernel,
        out_shape=jax.ShapeDtypeStruct(x.shape, x.dtype),
        grid=(batch, seq // ts),
        in_specs=[
            pl.BlockSpec((1, ts, d_model), lambda b, i: (b, i, 0)),
            pl.BlockSpec((1, HALO, d_model), lambda b, i: (
                b, jnp.maximum(i * blocks_per_tile - 1, 0), 0)),
            pl.BlockSpec((1, HALO, d_model), lambda b, i: (
                b, jnp.minimum((i + 1) * blocks_per_tile, n_halo_blocks - 1), 0)),
            const(1, d_model),
            const(d_model, in_cols),
            const(len(POOL_WINDOWS) // 2, 2 * pool_width // len(POOL_WINDOWS),
                  2 * pool_width // len(POOL_WINDOWS)),
            const(1, pool_width),
            const(3, conv_width),
            const(1, conv_width),
            const(mix_w, d_model),
            const(1, d_model),
        ],
        out_specs=pl.BlockSpec((1, ts, d_model), lambda b, i: (b, i, 0)),
        scratch_shapes=[pltpu.VMEM((ts + 2 * HALO, d_model), bf16)],
        compiler_params=pltpu.CompilerParams(
            dimension_semantics=("arbitrary", "arbitrary"),
            vmem_limit_bytes=V7X_VMEM_BYTES * 7 // 8),
        name="hybrid_block_layer",
    )(x, x, x, norm_g.reshape(1, -1), w_in.astype(bf16),
      _pair_block_diag(pool_w).astype(bf16), pool_scale.reshape(1, -1),
      conv_w, conv_b.reshape(1, -1), w_out.astype(bf16),
      final_g.reshape(1, -1))


def kernel(x, norm_g, w_in, pool_w, pool_scale, conv_w, conv_b, w_out, final_g):
    depth = norm_g.shape[0]
    for l in range(depth):
        x = _layer(x, norm_g[l], w_in[l], pool_w[l], pool_scale[l], conv_w[l],
                   conv_b[l], w_out[l], final_g, apply_final=(l == depth - 1))
    return x
```

```python
import functools

import jax
import jax.numpy as jnp
from jax import lax
from jax.experimental import pallas as pl
from jax.experimental.pallas import tpu as pltpu

POOL_WINDOWS = (2, 4, 8, 16)
EPS = 1e-6

LANES = 128
BF16_SUBLANES = 16
V7X_VMEM_BYTES = 64 * 1024 * 1024
SEQ_TILE = 1024
HALO = BF16_SUBLANES
OUT_ROWS = 256
N_STAGE = 2


def _rmsnorm(xf, g):
    ms = jnp.mean(xf * xf, axis=-1, keepdims=True)
    return xf * lax.rsqrt(ms + EPS) * g


def _silu(z):
    return z * jax.nn.sigmoid(z)


def _prepare_weights(win_hbm, wout_hbm, pw_ref, win_bf, wout_bf, pw_bd,
                     stage, sems, *, chunk):
    bf16 = jnp.bfloat16
    jobs = [(win_hbm, win_bf, c) for c in range(win_bf.shape[1] // chunk)]
    jobs += [(wout_hbm, wout_bf, c) for c in range(wout_bf.shape[1] // chunk)]

    def copy(k):
        src, _, c = jobs[k]
        return pltpu.make_async_copy(
            src.at[:, c * chunk:(c + 1) * chunk], stage.at[k % N_STAGE],
            sems.at[k % N_STAGE])

    for k in range(min(N_STAGE, len(jobs))):
        copy(k).start()
    for k, (_, dst, c) in enumerate(jobs):
        copy(k).wait()
        dst[:, c * chunk:(c + 1) * chunk] = stage[k % N_STAGE].astype(bf16)
        if k + N_STAGE < len(jobs):
            copy(k + N_STAGE).start()

    n_pairs, pair_w, _ = pw_bd.shape
    half = pair_w // 2
    pw_bd[...] = jnp.zeros(pw_bd.shape, bf16)
    for p in range(n_pairs):
        pw_bd[p, 0:half, 0:half] = pw_ref[2 * p].astype(bf16)
        pw_bd[p, half:, half:] = pw_ref[2 * p + 1].astype(bf16)


def _layer_kernel(x_ref, xp_ref, xn_ref, ng_ref, win_hbm, pw_ref, ps_ref,
                  cw_ref, cb_ref, wout_hbm, fg_ref, o_ref,
                  h_ref, win_ref, wout_ref, pwbd_ref, stage_ref, sems, *,
                  ts, seq, pool_w, apply_final):
    f32, bf16 = jnp.float32, jnp.bfloat16
    b = pl.program_id(0)
    i = pl.program_id(1)
    n_tiles = pl.num_programs(1)
    n_ext = ts + 2 * HALO
    group = pool_w // len(POOL_WINDOWS)

    @pl.when((b == 0) & (i == 0))
    def _():
        _prepare_weights(win_hbm, wout_hbm, pw_ref, win_ref, wout_ref,
                         pwbd_ref, stage_ref, sems, chunk=pool_w)

    g = ng_ref[...]
    has_prev = (i > 0).astype(f32)
    has_next = (i < n_tiles - 1).astype(f32)
    h_ref[0:HALO] = (_rmsnorm(xp_ref[0], g) * has_prev).astype(bf16)
    h_ref[HALO:HALO + ts] = _rmsnorm(x_ref[0], g).astype(bf16)
    h_ref[HALO + ts:n_ext] = (_rmsnorm(xn_ref[0], g) * has_next).astype(bf16)

    def proj(h, col):
        return jnp.dot(h, win_ref[:, col:col + pool_w],
                       preferred_element_type=f32)

    h_ext = h_ref[...]
    h_main = h_ref[HALO:HALO + ts]
    u = proj(h_ext, 0)
    z_a = proj(h_main, pool_w)
    b_g = proj(h_main, 2 * pool_w)
    c_g = proj(h_ext, 3 * pool_w)
    v_b = proj(h_ext, 4 * pool_w)
    z_b = proj(h_main, 5 * pool_w)

    def shifted(a, k):
        return pltpu.roll(a, (-k) % n_ext, axis=0)

    t = (i * ts + lax.broadcasted_iota(jnp.int32, (ts, group), 0)).astype(f32)
    pooled = []
    for gi, w in enumerate(POOL_WINDOWS):
        ug = u[:, gi * group:(gi + 1) * group]
        half = w // 2
        a = ug
        span = 1
        while span < half:
            a = a + shifted(a, span)
            span *= 2
        tot = shifted(a, -half) + a
        cnt = jnp.minimum(t + half, float(seq)) - jnp.maximum(t - half, 0.0)
        pooled.append(tot[HALO:HALO + ts] / cnt - ug[HALO:HALO + ts])
    pooled = jnp.concatenate(pooled, axis=-1).astype(bf16)
    pair = 2 * group
    mixed = jnp.concatenate(
        [jnp.dot(pooled[:, p * pair:(p + 1) * pair], pwbd_ref[p],
                 preferred_element_type=f32)
         for p in range(pool_w // pair)], axis=-1)
    y_a = (mixed * ps_ref[...] * _silu(z_a)).astype(bf16)

    cv = c_g * v_b
    conv = (shifted(cv, -1) * cw_ref[0:1] + cv * cw_ref[1:2]
            + shifted(cv, 1) * cw_ref[2:3] + cb_ref[...])
    y_b = (b_g * conv[HALO:HALO + ts] * _silu(z_b)).astype(bf16)

    for r0 in range(0, ts, OUT_ROWS):
        rows = slice(r0, r0 + OUT_ROWS)
        out = x_ref[0, rows] + (
            jnp.dot(y_a[rows], wout_ref[0:pool_w], preferred_element_type=f32)
            + jnp.dot(y_b[rows], wout_ref[pool_w:], preferred_element_type=f32))
        if apply_final:
            out = _rmsnorm(out, fg_ref[...])
        o_ref[0, rows] = out


def _layer(x, norm_g, w_in, pool_w, pool_scale, conv_w, conv_b, w_out,
           final_g, *, apply_final):
    batch, seq, d_model = x.shape
    in_cols = w_in.shape[1]
    mix_w = w_out.shape[0]
    pool_width = pool_scale.shape[0]
    conv_width = conv_b.shape[0]
    n_groups, group, _ = pool_w.shape
    ts = SEQ_TILE
    assert seq % ts == 0 and ts % HALO == 0 and ts % OUT_ROWS == 0
    assert pool_width == conv_width and in_cols == 6 * pool_width
    assert mix_w == pool_width + conv_width and mix_w == d_model
    assert n_groups == len(POOL_WINDOWS) and n_groups * group == pool_width
    assert group % LANES == 0 and n_groups % 2 == 0
    assert max(POOL_WINDOWS) // 2 <= HALO
    blocks_per_tile = ts // HALO
    n_halo_blocks = seq // HALO

    f32, bf16 = jnp.float32, jnp.bfloat16
    const = lambda *shape: pl.BlockSpec(
        shape, lambda b, i: (0,) * len(shape), pipeline_mode=pl.Buffered(1))
    hbm = pl.BlockSpec(memory_space=pl.ANY)
    kernel = functools.partial(_layer_kernel, ts=ts, seq=seq,
                               pool_w=pool_width, apply_final=apply_final)
    return pl.pallas_call(
        kernel,
        out_shape=jax.ShapeDtypeStruct(x.shape, x.dtype),
        grid=(batch, seq // ts),
        in_specs=[
            pl.BlockSpec((1, ts, d_model), lambda b, i: (b, i, 0)),
            pl.BlockSpec((1, HALO, d_model), lambda b, i: (
                b, jnp.maximum(i * blocks_per_tile - 1, 0), 0)),
            pl.BlockSpec((1, HALO, d_model), lambda b, i: (
                b, jnp.minimum((i + 1) * blocks_per_tile, n_halo_blocks - 1), 0)),
            const(1, d_model),
            hbm,
            const(n_groups, group, group),
            const(1, pool_width),
            const(3, conv_width),
            const(1, conv_width),
            hbm,
            const(1, d_model),
        ],
        out_specs=pl.BlockSpec((1, ts, d_model), lambda b, i: (b, i, 0)),
        scratch_shapes=[
            pltpu.VMEM((ts + 2 * HALO, d_model), bf16),
            pltpu.VMEM((d_model, in_cols), bf16),
            pltpu.VMEM((mix_w, d_model), bf16),
            pltpu.VMEM((n_groups // 2, 2 * group, 2 * group), bf16),
            pltpu.VMEM((N_STAGE, d_model, pool_width), f32),
            pltpu.SemaphoreType.DMA((N_STAGE,)),
        ],
        compiler_params=pltpu.CompilerParams(
            dimension_semantics=("arbitrary", "arbitrary"),
            vmem_limit_bytes=V7X_VMEM_BYTES * 7 // 8),
        name="hybrid_block_layer",
    )(x, x, x, norm_g.reshape(1, -1), w_in, pool_w, pool_scale.reshape(1, -1),
      conv_w, conv_b.reshape(1, -1), w_out, final_g.reshape(1, -1))


def kernel(x, norm_g, w_in, pool_w, pool_scale, conv_w, conv_b, w_out, final_g):
    depth = norm_g.shape[0]
    for l in range(depth):
        x = _layer(x, norm_g[l], w_in[l], pool_w[l], pool_scale[l], conv_w[l],
                   conv_b[l], w_out[l], final_g, apply_final=(l == depth - 1))
    return x
```

```python
import functools

import jax
import jax.numpy as jnp
from jax import lax
from jax.experimental import pallas as pl
from jax.experimental.pallas import tpu as pltpu

POOL_WINDOWS = (2, 4, 8, 16)
EPS = 1e-6

LANES = 128
SUBLANES = 8
BF16_SUBLANES = 16
V7X_VMEM_BYTES = 64 * 1024 * 1024
SEQ_TILE = 1024
HALO = BF16_SUBLANES
OUT_ROWS = 256
N_STAGE = 2
SUM_PAD = SUBLANES
N_LEVELS = sum(max((w // 2).bit_length() - 1, 0) for w in POOL_WINDOWS)


def _rmsnorm(xf, g):
    ms = jnp.mean(xf * xf, axis=-1, keepdims=True)
    return xf * lax.rsqrt(ms + EPS) * g


def _silu(z):
    return z * jax.nn.sigmoid(z)


def _prepare(win_hbm, wout_hbm, pw_ref, win_bf, wout_bf, pw_bd, stage, sems,
             u_s, lvl_s, *, chunk, n_ext):
    bf16 = jnp.bfloat16
    jobs = [(win_hbm, win_bf, c) for c in range(win_bf.shape[1] // chunk)]
    jobs += [(wout_hbm, wout_bf, c) for c in range(wout_bf.shape[1] // chunk)]

    def copy(k):
        src, _, c = jobs[k]
        return pltpu.make_async_copy(
            src.at[:, c * chunk:(c + 1) * chunk], stage.at[k % N_STAGE],
            sems.at[k % N_STAGE])

    for k in range(min(N_STAGE, len(jobs))):
        copy(k).start()
    for k, (_, dst, c) in enumerate(jobs):
        copy(k).wait()
        dst[:, c * chunk:(c + 1) * chunk] = stage[k % N_STAGE].astype(bf16)
        if k + N_STAGE < len(jobs):
            copy(k + N_STAGE).start()

    n_pairs, pair_w, _ = pw_bd.shape
    half = pair_w // 2
    pw_bd[...] = jnp.zeros(pw_bd.shape, bf16)
    for p in range(n_pairs):
        pw_bd[p, 0:half, 0:half] = pw_ref[2 * p].astype(bf16)
        pw_bd[p, half:, half:] = pw_ref[2 * p + 1].astype(bf16)

    for ref in (u_s, lvl_s):
        ref[:, n_ext:] = jnp.zeros(
            (ref.shape[0], SUM_PAD, ref.shape[2]), ref.dtype)


def _layer_kernel(x_ref, xp_ref, xn_ref, ng_ref, win_hbm, pw_ref, ps_ref,
                  cw_ref, cb_ref, wout_hbm, fg_ref, o_ref,
                  h_ref, win_ref, wout_ref, pwbd_ref, stage_ref, sems,
                  u_s, lvl_s, cv_s, *, ts, seq, pool_w, apply_final):
    f32, bf16 = jnp.float32, jnp.bfloat16
    b = pl.program_id(0)
    i = pl.program_id(1)
    n_tiles = pl.num_programs(1)
    n_ext = ts + 2 * HALO
    group = pool_w // len(POOL_WINDOWS)
    main = slice(HALO, HALO + ts)

    @pl.when((b == 0) & (i == 0))
    def _():
        _prepare(win_hbm, wout_hbm, pw_ref, win_ref, wout_ref, pwbd_ref,
                 stage_ref, sems, u_s, lvl_s, chunk=pool_w, n_ext=n_ext)

    g = ng_ref[...]
    has_prev = (i > 0).astype(f32)
    has_next = (i < n_tiles - 1).astype(f32)
    h_ref[0:HALO] = (_rmsnorm(xp_ref[0], g) * has_prev).astype(bf16)
    h_ref[main] = _rmsnorm(x_ref[0], g).astype(bf16)
    h_ref[HALO + ts:n_ext] = (_rmsnorm(xn_ref[0], g) * has_next).astype(bf16)

    def proj(h, col):
        return jnp.dot(h, win_ref[:, col:col + pool_w],
                       preferred_element_type=f32)

    def window_mean(tot, half):
        row = lax.broadcasted_iota(jnp.int32, (SUBLANES, group), 0)

        def edge(r0):
            t = (i * ts + r0 + row).astype(f32)
            cnt = (jnp.minimum(t + half, float(seq))
                   - jnp.maximum(t - half, 0.0))
            return tot[r0:r0 + SUBLANES] / cnt

        inner = tot[SUBLANES:ts - SUBLANES] * (1.0 / (2 * half))
        return jnp.concatenate([edge(0), inner, edge(ts - SUBLANES)], axis=0)

    h_ext = h_ref[...]
    h_main = h_ref[main]
    u = proj(h_ext, 0)
    for gi in range(len(POOL_WINDOWS)):
        u_s[gi, 0:n_ext] = u[:, gi * group:(gi + 1) * group]
    c_g = proj(h_ext, 3 * pool_w)
    v_b = proj(h_ext, 4 * pool_w)

    lo, n_sum = SUBLANES, n_ext - SUBLANES
    pooled = []
    level = 0
    for gi, w in enumerate(POOL_WINDOWS):
        half = w // 2

        def read(start, n, src=None, gi=gi):
            if src is None:
                return u_s[gi, start:start + n]
            return lvl_s[src, start:start + n]

        src = None
        span = 1
        while span < half:
            lvl_s[level, lo:lo + n_sum] = (
                read(lo, n_sum, src) + read(lo + span, n_sum, src))
            src = level
            level += 1
            span *= 2
        tot = read(HALO - half, ts, src) + read(HALO, ts, src)
        pooled.append(window_mean(tot, half) - u_s[gi, main])
    pooled = jnp.concatenate(pooled, axis=-1).astype(bf16)

    z_b = proj(h_main, 5 * pool_w)

    conv_rows = slice(HALO - SUBLANES, HALO + ts + SUBLANES)
    conv = []
    for p in range(pool_w // LANES):
        cols = slice(p * LANES, (p + 1) * LANES)
        cv_s[p, conv_rows] = c_g[conv_rows, cols] * v_b[conv_rows, cols]
        conv.append(cv_s[p, HALO - 1:HALO - 1 + ts] * cw_ref[0:1, cols]
                    + cv_s[p, main] * cw_ref[1:2, cols]
                    + cv_s[p, HALO + 1:HALO + 1 + ts] * cw_ref[2:3, cols]
                    + cb_ref[:, cols])
    conv = jnp.concatenate(conv, axis=-1)

    z_a = proj(h_main, pool_w)
    gate_b = _silu(z_b)

    pair = 2 * group
    mixed = jnp.concatenate(
        [jnp.dot(pooled[:, p * pair:(p + 1) * pair], pwbd_ref[p],
                 preferred_element_type=f32)
         for p in range(pool_w // pair)], axis=-1)
    gate_a = _silu(z_a)

    b_g = proj(h_main, 2 * pool_w)
    y_a = (mixed * ps_ref[...] * gate_a).astype(bf16)

    for r0 in range(0, ts, OUT_ROWS):
        rows = slice(r0, r0 + OUT_ROWS)
        o_ref[0, rows] = x_ref[0, rows] + jnp.dot(
            y_a[rows], wout_ref[0:pool_w], preferred_element_type=f32)
    y_b = (b_g * conv * gate_b).astype(bf16)
    for r0 in range(0, ts, OUT_ROWS):
        rows = slice(r0, r0 + OUT_ROWS)
        out = o_ref[0, rows] + jnp.dot(
            y_b[rows], wout_ref[pool_w:], preferred_element_type=f32)
        if apply_final:
            out = _rmsnorm(out, fg_ref[...])
        o_ref[0, rows] = out


def _layer(x, norm_g, w_in, pool_w, pool_scale, conv_w, conv_b, w_out,
           final_g, *, apply_final):
    batch, seq, d_model = x.shape
    in_cols = w_in.shape[1]
    mix_w = w_out.shape[0]
    pool_width = pool_scale.shape[0]
    conv_width = conv_b.shape[0]
    n_groups, group, _ = pool_w.shape
    ts = SEQ_TILE
    assert seq % ts == 0 and ts % HALO == 0 and ts % OUT_ROWS == 0
    assert pool_width == conv_width and in_cols == 6 * pool_width
    assert mix_w == pool_width + conv_width and mix_w == d_model
    assert n_groups == len(POOL_WINDOWS) and n_groups * group == pool_width
    assert group % LANES == 0 and n_groups % 2 == 0
    assert max(POOL_WINDOWS) // 2 <= SUBLANES <= HALO
    assert max(POOL_WINDOWS) // 4 <= SUM_PAD
    blocks_per_tile = ts // HALO
    n_halo_blocks = seq // HALO
    n_ext = ts + 2 * HALO

    f32, bf16 = jnp.float32, jnp.bfloat16
    const = lambda *shape: pl.BlockSpec(
        shape, lambda b, i: (0,) * len(shape), pipeline_mode=pl.Buffered(1))
    hbm = pl.BlockSpec(memory_space=pl.ANY)
    kernel = functools.partial(_layer_kernel, ts=ts, seq=seq,
                               pool_w=pool_width, apply_final=apply_final)
    return pl.pallas_call(
        kernel,
        out_shape=jax.ShapeDtypeStruct(x.shape, x.dtype),
        grid=(batch, seq // ts),
        in_specs=[
            pl.BlockSpec((1, ts, d_model), lambda b, i: (b, i, 0)),
            pl.BlockSpec((1, HALO, d_model), lambda b, i: (
                b, jnp.maximum(i * blocks_per_tile - 1, 0), 0)),
            pl.BlockSpec((1, HALO, d_model), lambda b, i: (
                b, jnp.minimum((i + 1) * blocks_per_tile, n_halo_blocks - 1), 0)),
            const(1, d_model),
            hbm,
            const(n_groups, group, group),
            const(1, pool_width),
            const(3, conv_width),
            const(1, conv_width),
            hbm,
            const(1, d_model),
        ],
        out_specs=pl.BlockSpec((1, ts, d_model), lambda b, i: (b, i, 0)),
        scratch_shapes=[
            pltpu.VMEM((n_ext, d_model), bf16),
            pltpu.VMEM((d_model, in_cols), bf16),
            pltpu.VMEM((mix_w, d_model), bf16),
            pltpu.VMEM((n_groups // 2, 2 * group, 2 * group), bf16),
            pltpu.VMEM((N_STAGE, d_model, pool_width), f32),
            pltpu.SemaphoreType.DMA((N_STAGE,)),
            pltpu.VMEM((n_groups, n_ext + SUM_PAD, group), f32),
            pltpu.VMEM((N_LEVELS, n_ext + SUM_PAD, group), f32),
            pltpu.VMEM((conv_width // LANES, n_ext, LANES), f32),
        ],
        compiler_params=pltpu.CompilerParams(
            dimension_semantics=("arbitrary", "arbitrary"),
            vmem_limit_bytes=V7X_VMEM_BYTES * 7 // 8),
        name="hybrid_block_layer",
    )(x, x, x, norm_g.reshape(1, -1), w_in, pool_w, pool_scale.reshape(1, -1),
      conv_w, conv_b.reshape(1, -1), w_out, final_g.reshape(1, -1))


def kernel(x, norm_g, w_in, pool_w, pool_scale, conv_w, conv_b, w_out, final_g):
    depth = norm_g.shape[0]
    for l in range(depth):
        x = _layer(x, norm_g[l], w_in[l], pool_w[l], pool_scale[l], conv_w[l],
                   conv_b[l], w_out[l], final_g, apply_final=(l == depth - 1))
    return x
```

```python
import functools

import jax
import jax.numpy as jnp
from jax import lax
from jax.experimental import pallas as pl
from jax.experimental.pallas import tpu as pltpu

POOL_WINDOWS = (2, 4, 8, 16)
EPS = 1e-6

LANES = 128
SUBLANES = 8
BF16_SUBLANES = 16
V7X_VMEM_BYTES = 64 * 1024 * 1024
SEQ_TILE = 1024
HALO = BF16_SUBLANES
OUT_ROWS = 256
N_STAGE = 2


def _rmsnorm(xf, g):
    ms = jnp.mean(xf * xf, axis=-1, keepdims=True)
    return xf * lax.rsqrt(ms + EPS) * g


def _silu(z):
    return z * jax.nn.sigmoid(z)


def _prepare_weights(win_hbm, wout_hbm, pw_ref, win_bf, wout_bf, pw_bd,
                     stage, sems, *, chunk):
    bf16 = jnp.bfloat16
    jobs = [(win_hbm, win_bf, c) for c in range(win_bf.shape[1] // chunk)]
    jobs += [(wout_hbm, wout_bf, c) for c in range(wout_bf.shape[1] // chunk)]

    def copy(k):
        src, _, c = jobs[k]
        return pltpu.make_async_copy(
            src.at[:, c * chunk:(c + 1) * chunk], stage.at[k % N_STAGE],
            sems.at[k % N_STAGE])

    for k in range(min(N_STAGE, len(jobs))):
        copy(k).start()
    for k, (_, dst, c) in enumerate(jobs):
        copy(k).wait()
        dst[:, c * chunk:(c + 1) * chunk] = stage[k % N_STAGE].astype(bf16)
        if k + N_STAGE < len(jobs):
            copy(k + N_STAGE).start()

    n_pairs, pair_w, _ = pw_bd.shape
    half = pair_w // 2
    pw_bd[...] = jnp.zeros(pw_bd.shape, bf16)
    for p in range(n_pairs):
        pw_bd[p, 0:half, 0:half] = pw_ref[2 * p].astype(bf16)
        pw_bd[p, half:, half:] = pw_ref[2 * p + 1].astype(bf16)


def _layer_kernel(x_ref, xp_ref, xn_ref, ng_ref, win_hbm, pw_ref, ps_ref,
                  cw_ref, cb_ref, wout_hbm, fg_ref, o_ref,
                  h_ref, win_ref, wout_ref, pwbd_ref, stage_ref, sems, *,
                  ts, seq, pool_w, apply_final):
    f32, bf16 = jnp.float32, jnp.bfloat16
    b = pl.program_id(0)
    i = pl.program_id(1)
    n_tiles = pl.num_programs(1)
    n_ext = ts + 2 * HALO
    group = pool_w // len(POOL_WINDOWS)

    @pl.when((b == 0) & (i == 0))
    def _():
        _prepare_weights(win_hbm, wout_hbm, pw_ref, win_ref, wout_ref,
                         pwbd_ref, stage_ref, sems, chunk=pool_w)

    g = ng_ref[...]
    has_prev = (i > 0).astype(f32)
    has_next = (i < n_tiles - 1).astype(f32)
    h_ref[0:HALO] = (_rmsnorm(xp_ref[0], g) * has_prev).astype(bf16)
    h_ref[HALO:HALO + ts] = _rmsnorm(x_ref[0], g).astype(bf16)
    h_ref[HALO + ts:n_ext] = (_rmsnorm(xn_ref[0], g) * has_next).astype(bf16)

    def proj(h, col):
        return jnp.dot(h, win_ref[:, col:col + pool_w],
                       preferred_element_type=f32)

    def shifted(a, k):
        return pltpu.roll(a, (-k) % n_ext, axis=0)

    def window_mean(tot, half):
        row = lax.broadcasted_iota(jnp.int32, (SUBLANES, group), 0)

        def edge(r0):
            t = (i * ts + r0 + row).astype(f32)
            cnt = (jnp.minimum(t + half, float(seq))
                   - jnp.maximum(t - half, 0.0))
            return tot[r0:r0 + SUBLANES] / cnt

        inner = tot[SUBLANES:ts - SUBLANES] * (1.0 / (2 * half))
        return jnp.concatenate([edge(0), inner, edge(ts - SUBLANES)], axis=0)

    h_ext = h_ref[...]
    h_main = h_ref[HALO:HALO + ts]
    u = proj(h_ext, 0)
    c_g = proj(h_ext, 3 * pool_w)
    v_b = proj(h_ext, 4 * pool_w)
    z_b = proj(h_main, 5 * pool_w)
    z_a = proj(h_main, pool_w)
    b_g = proj(h_main, 2 * pool_w)

    pooled = []
    for gi, w in enumerate(POOL_WINDOWS):
        ug = u[:, gi * group:(gi + 1) * group]
        half = w // 2
        a = ug
        span = 1
        while span < half:
            a = a + shifted(a, span)
            span *= 2
        tot = shifted(a, -half) + a
        pooled.append(window_mean(tot[HALO:HALO + ts], half)
                      - ug[HALO:HALO + ts])
    pooled = jnp.concatenate(pooled, axis=-1).astype(bf16)
    pair = 2 * group
    mixed = jnp.concatenate(
        [jnp.dot(pooled[:, p * pair:(p + 1) * pair], pwbd_ref[p],
                 preferred_element_type=f32)
         for p in range(pool_w // pair)], axis=-1)
    y_a = (mixed * ps_ref[...] * _silu(z_a)).astype(bf16)

    cv = c_g * v_b
    conv = (shifted(cv, -1) * cw_ref[0:1] + cv * cw_ref[1:2]
            + shifted(cv, 1) * cw_ref[2:3] + cb_ref[...])
    y_b = (b_g * conv[HALO:HALO + ts] * _silu(z_b)).astype(bf16)

    for r0 in range(0, ts, OUT_ROWS):
        rows = slice(r0, r0 + OUT_ROWS)
        out = x_ref[0, rows] + (
            jnp.dot(y_a[rows], wout_ref[0:pool_w], preferred_element_type=f32)
            + jnp.dot(y_b[rows], wout_ref[pool_w:], preferred_element_type=f32))
        if apply_final:
            out = _rmsnorm(out, fg_ref[...])
        o_ref[0, rows] = out


def _layer(x, norm_g, w_in, pool_w, pool_scale, conv_w, conv_b, w_out,
           final_g, *, apply_final):
    batch, seq, d_model = x.shape
    in_cols = w_in.shape[1]
    mix_w = w_out.shape[0]
    pool_width = pool_scale.shape[0]
    conv_width = conv_b.shape[0]
    n_groups, group, _ = pool_w.shape
    ts = SEQ_TILE
    assert seq % ts == 0 and ts % HALO == 0 and ts % OUT_ROWS == 0
    assert pool_width == conv_width and in_cols == 6 * pool_width
    assert mix_w == pool_width + conv_width and mix_w == d_model
    assert n_groups == len(POOL_WINDOWS) and n_groups * group == pool_width
    assert group % LANES == 0 and n_groups % 2 == 0
    assert max(POOL_WINDOWS) // 2 <= HALO
    blocks_per_tile = ts // HALO
    n_halo_blocks = seq // HALO

    f32, bf16 = jnp.float32, jnp.bfloat16
    const = lambda *shape: pl.BlockSpec(
        shape, lambda b, i: (0,) * len(shape), pipeline_mode=pl.Buffered(1))
    hbm = pl.BlockSpec(memory_space=pl.ANY)
    kernel = functools.partial(_layer_kernel, ts=ts, seq=seq,
                               pool_w=pool_width, apply_final=apply_final)
    return pl.pallas_call(
        kernel,
        out_shape=jax.ShapeDtypeStruct(x.shape, x.dtype),
        grid=(batch, seq // ts),
        in_specs=[
            pl.BlockSpec((1, ts, d_model), lambda b, i: (b, i, 0)),
            pl.BlockSpec((1, HALO, d_model), lambda b, i: (
                b, jnp.maximum(i * blocks_per_tile - 1, 0), 0)),
            pl.BlockSpec((1, HALO, d_model), lambda b, i: (
                b, jnp.minimum((i + 1) * blocks_per_tile, n_halo_blocks - 1), 0)),
            const(1, d_model),
            hbm,
            const(n_groups, group, group),
            const(1, pool_width),
            const(3, conv_width),
            const(1, conv_width),
            hbm,
            const(1, d_model),
        ],
        out_specs=pl.BlockSpec((1, ts, d_model), lambda b, i: (b, i, 0)),
        scratch_shapes=[
            pltpu.VMEM((ts + 2 * HALO, d_model), bf16),
            pltpu.VMEM((d_model, in_cols), bf16),
            pltpu.VMEM((mix_w, d_model), bf16),
            pltpu.VMEM((n_groups // 2, 2 * group, 2 * group), bf16),
            pltpu.VMEM((N_STAGE, d_model, pool_width), f32),
            pltpu.SemaphoreType.DMA((N_STAGE,)),
        ],
        compiler_params=pltpu.CompilerParams(
            dimension_semantics=("arbitrary", "arbitrary"),
            vmem_limit_bytes=V7X_VMEM_BYTES * 7 // 8),
        name="hybrid_block_layer",
    )(x, x, x, norm_g.reshape(1, -1), w_in, pool_w, pool_scale.reshape(1, -1),
      conv_w, conv_b.reshape(1, -1), w_out, final_g.reshape(1, -1))


def kernel(x, norm_g, w_in, pool_w, pool_scale, conv_w, conv_b, w_out, final_g):
    depth = norm_g.shape[0]
    for l in range(depth):
        x = _layer(x, norm_g[l], w_in[l], pool_w[l], pool_scale[l], conv_w[l],
                   conv_b[l], w_out[l], final_g, apply_final=(l == depth - 1))
    return x
```

```python
import functools

import jax
import jax.numpy as jnp
from jax import lax
from jax.experimental import pallas as pl
from jax.experimental.pallas import tpu as pltpu

POOL_WINDOWS = (2, 4, 8, 16)
EPS = 1e-6

LANES = 128
SUBLANES = 8
BF16_SUBLANES = 16
V7X_VMEM_BYTES = 64 * 1024 * 1024
SEQ_TILE = 1024
HALO = BF16_SUBLANES
OUT_ROWS = 256
N_STAGE = 4
SUM_PAD = 32
N_LEVELS = sum(max((w // 2).bit_length() - 1, 0) for w in POOL_WINDOWS)


def _rmsnorm(xf, g):
    ms = jnp.mean(xf * xf, axis=-1, keepdims=True)
    return xf * lax.rsqrt(ms + EPS) * g


def _silu(z):
    return z * jax.nn.sigmoid(z)


def _prepare_weights(win_hbm, wout_hbm, pw_ref, win_bf, wout_bf,
                     stage, sems, *, chunk):
    f32, bf16 = jnp.float32, jnp.bfloat16
    jobs = [(win_hbm, win_bf, c) for c in range(win_bf.shape[1] // chunk)]
    jobs += [(wout_hbm, wout_bf, c) for c in range(wout_bf.shape[1] // chunk)]
    n_groups, group, _ = pw_ref.shape

    def copy(k):
        src, _, c = jobs[k]
        return pltpu.make_async_copy(
            src.at[:, c * chunk:(c + 1) * chunk], stage.at[k % N_STAGE],
            sems.at[k % N_STAGE])

    for k in range(min(N_STAGE, len(jobs))):
        copy(k).start()
    for k, (src, dst, c) in enumerate(jobs):
        copy(k).wait()
        w = stage[k % N_STAGE]
        if src is win_hbm and c == 0:
            w = jnp.concatenate(
                [jnp.dot(w[:, gi * group:(gi + 1) * group], pw_ref[gi],
                         precision=lax.Precision.HIGHEST,
                         preferred_element_type=f32)
                 for gi in range(n_groups)], axis=-1)
        dst[:, c * chunk:(c + 1) * chunk] = w.astype(bf16)
        if k + N_STAGE < len(jobs):
            copy(k + N_STAGE).start()


def _layer_kernel(x_ref, xp_ref, xn_ref, ng_ref, win_hbm, pw_ref, ps_ref,
                  cw_ref, cb_ref, wout_hbm, fg_ref, o_ref,
                  h_ref, win_ref, wout_ref, stage_ref, sems,
                  u_s, lvl_s, cv_s, *, ts, seq, pool_w, apply_final):
    f32, bf16 = jnp.float32, jnp.bfloat16
    b = pl.program_id(0)
    i = pl.program_id(1)
    n_tiles = pl.num_programs(1)
    n_ext = ts + 2 * HALO
    group = pool_w // len(POOL_WINDOWS)

    @pl.when((b == 0) & (i == 0))
    def _():
        _prepare_weights(win_hbm, wout_hbm, pw_ref, win_ref, wout_ref,
                         stage_ref, sems, chunk=pool_w)
        for ref in (u_s, lvl_s):
            ref[:, n_ext:] = jnp.zeros(
                (ref.shape[0], SUM_PAD, ref.shape[2]), ref.dtype)

    g = ng_ref[...]
    has_prev = (i > 0).astype(f32)
    has_next = (i < n_tiles - 1).astype(f32)
    h_ref[0:HALO] = (_rmsnorm(xp_ref[0], g) * has_prev).astype(bf16)
    h_ref[HALO:HALO + ts] = _rmsnorm(x_ref[0], g).astype(bf16)
    h_ref[HALO + ts:n_ext] = (_rmsnorm(xn_ref[0], g) * has_next).astype(bf16)

    def proj(h, col):
        return jnp.dot(h, win_ref[:, col:col + pool_w],
                       preferred_element_type=f32)

    def window_mean(tot, half):
        row = lax.broadcasted_iota(jnp.int32, (SUBLANES, group), 0)

        def edge(r0):
            t = (i * ts + r0 + row).astype(f32)
            cnt = (jnp.minimum(t + half, float(seq))
                   - jnp.maximum(t - half, 0.0))
            return tot[r0:r0 + SUBLANES] / cnt

        inner = tot[SUBLANES:ts - SUBLANES] * (1.0 / (2 * half))
        return jnp.concatenate([edge(0), inner, edge(ts - SUBLANES)], axis=0)

    h_ext = h_ref[...]
    h_main = h_ref[HALO:HALO + ts]
    u = proj(h_ext, 0)
    c_g = proj(h_ext, 3 * pool_w)
    v_b = proj(h_ext, 4 * pool_w)
    z_b = proj(h_main, 5 * pool_w)
    z_a = proj(h_main, pool_w)
    b_g = proj(h_main, 2 * pool_w)

    main = slice(HALO, HALO + ts)
    for gi in range(len(POOL_WINDOWS)):
        u_s[gi, 0:n_ext] = u[:, gi * group:(gi + 1) * group]
    lo, n_sum = SUBLANES, n_ext - SUBLANES
    pooled = []
    level = 0
    for gi, w in enumerate(POOL_WINDOWS):
        half = w // 2

        def read(start, n, src=None, gi=gi):
            if src is None:
                return u_s[gi, start:start + n]
            return lvl_s[src, start:start + n]

        src = None
        span = 1
        while span < half:
            lvl_s[level, lo:lo + n_sum] = (
                read(lo, n_sum, src) + read(lo + span, n_sum, src))
            src = level
            level += 1
            span *= 2
        tot = read(HALO - half, ts, src) + read(HALO, ts, src)
        pooled.append(window_mean(tot, half) - u_s[gi, main])
    mixed = jnp.concatenate(pooled, axis=-1)

    conv_rows = slice(HALO - SUBLANES, HALO + ts + SUBLANES)
    conv = []
    for p in range(pool_w // LANES):
        cols = slice(p * LANES, (p + 1) * LANES)
        cv_s[p, conv_rows] = c_g[conv_rows, cols] * v_b[conv_rows, cols]
        conv.append(cv_s[p, HALO - 1:HALO - 1 + ts] * cw_ref[0:1, cols]
                    + cv_s[p, main] * cw_ref[1:2, cols]
                    + cv_s[p, HALO + 1:HALO + 1 + ts] * cw_ref[2:3, cols]
                    + cb_ref[:, cols])
    conv = jnp.concatenate(conv, axis=-1)
    gate_b = _silu(z_b)
    y_a = (mixed * ps_ref[...] * _silu(z_a)).astype(bf16)
    y_b = (b_g * conv * gate_b).astype(bf16)

    for r0 in range(0, ts, OUT_ROWS):
        rows = slice(r0, r0 + OUT_ROWS)
        out = x_ref[0, rows] + (
            jnp.dot(y_a[rows], wout_ref[0:pool_w], preferred_element_type=f32)
            + jnp.dot(y_b[rows], wout_ref[pool_w:], preferred_element_type=f32))
        if apply_final:
            out = _rmsnorm(out, fg_ref[...])
        o_ref[0, rows] = out


def _layer(x, norm_g, w_in, pool_w, pool_scale, conv_w, conv_b, w_out,
           final_g, *, apply_final):
    batch, seq, d_model = x.shape
    in_cols = w_in.shape[1]
    mix_w = w_out.shape[0]
    pool_width = pool_scale.shape[0]
    conv_width = conv_b.shape[0]
    n_groups, group, _ = pool_w.shape
    ts = SEQ_TILE
    assert seq % ts == 0 and ts % HALO == 0 and ts % OUT_ROWS == 0
    assert pool_width == conv_width and in_cols == 6 * pool_width
    assert mix_w == pool_width + conv_width and mix_w == d_model
    assert n_groups == len(POOL_WINDOWS) and n_groups * group == pool_width
    assert group % LANES == 0 and n_groups % 2 == 0
    assert max(POOL_WINDOWS) // 2 <= HALO
    blocks_per_tile = ts // HALO
    n_halo_blocks = seq // HALO
    n_ext = ts + 2 * HALO
    assert max(POOL_WINDOWS) // 2 <= SUBLANES and max(POOL_WINDOWS) // 4 <= SUM_PAD

    f32, bf16 = jnp.float32, jnp.bfloat16
    const = lambda *shape: pl.BlockSpec(
        shape, lambda b, i: (0,) * len(shape), pipeline_mode=pl.Buffered(1))
    hbm = pl.BlockSpec(memory_space=pl.ANY)
    kernel = functools.partial(_layer_kernel, ts=ts, seq=seq,
                               pool_w=pool_width, apply_final=apply_final)
    return pl.pallas_call(
        kernel,
        out_shape=jax.ShapeDtypeStruct(x.shape, x.dtype),
        grid=(batch, seq // ts),
        in_specs=[
            pl.BlockSpec((1, ts, d_model), lambda b, i: (b, i, 0)),
            pl.BlockSpec((1, HALO, d_model), lambda b, i: (
                b, jnp.maximum(i * blocks_per_tile - 1, 0), 0)),
            pl.BlockSpec((1, HALO, d_model), lambda b, i: (
                b, jnp.minimum((i + 1) * blocks_per_tile, n_halo_blocks - 1), 0)),
            const(1, d_model),
            hbm,
            const(n_groups, group, group),
            const(1, pool_width),
            const(3, conv_width),
            const(1, conv_width),
            hbm,
            const(1, d_model),
        ],
        out_specs=pl.BlockSpec((1, ts, d_model), lambda b, i: (b, i, 0)),
        scratch_shapes=[
            pltpu.VMEM((ts + 2 * HALO, d_model), bf16),
            pltpu.VMEM((d_model, in_cols), bf16),
            pltpu.VMEM((mix_w, d_model), bf16),
            pltpu.VMEM((N_STAGE, d_model, pool_width), f32),
            pltpu.SemaphoreType.DMA((N_STAGE,)),
            pltpu.VMEM((n_groups, n_ext + SUM_PAD, group), f32),
            pltpu.VMEM((N_LEVELS, n_ext + SUM_PAD, group), f32),
            pltpu.VMEM((conv_width // LANES, n_ext, LANES), f32),
        ],
        compiler_params=pltpu.CompilerParams(
            dimension_semantics=("arbitrary", "arbitrary"),
            vmem_limit_bytes=V7X_VMEM_BYTES * 7 // 8),
        name="hybrid_block_layer",
    )(x, x, x, norm_g.reshape(1, -1), w_in, pool_w, pool_scale.reshape(1, -1),
      conv_w, conv_b.reshape(1, -1), w_out, final_g.reshape(1, -1))


def kernel(x, norm_g, w_in, pool_w, pool_scale, conv_w, conv_b, w_out, final_g):
    depth = norm_g.shape[0]
    for l in range(depth):
        x = _layer(x, norm_g[l], w_in[l], pool_w[l], pool_scale[l], conv_w[l],
                   conv_b[l], w_out[l], final_g, apply_final=(l == depth - 1))
    return x
```
